```python
import math
import jax
import jax.numpy as jnp
from jax import lax
import numpy as np

D_MODEL = 1024
BATCH = 4
SEQ = 4096
DEPTH = 4

CHUNK = 64
Q_BLOCK = 128
N_MIXERS = 3
EPS = 1e-6
NEG_INF = -1e30
D_FF = 4 * D_MODEL
N_MOD = 6

GLA_HEADS = 4
GLA_DK = D_MODEL // (2 * GLA_HEADS)
GLA_DV = D_MODEL // GLA_HEADS
GLA_GATE_RANK = 16
GLA_GATE_TEMP = 16.0
GLA_KW = GLA_HEADS * GLA_DK
GLA_VW = GLA_HEADS * GLA_DV
GLA_IN = 2 * GLA_KW + 2 * GLA_VW + GLA_GATE_RANK
GLA_SPLITS = [GLA_KW, 2 * GLA_KW, 2 * GLA_KW + GLA_VW, 2 * GLA_KW + 2 * GLA_VW]

MLA_HEADS = 16
MLA_NOPE = 64
MLA_ROPE = 32
MLA_V = 64
MLA_Q_RANK = 384
MLA_KV_RANK = 256
MLA_IN = MLA_Q_RANK + MLA_KV_RANK + MLA_ROPE
ROPE_THETA = 10000.0

GDN_K_HEADS = 8
GDN_V_HEADS = 16
GDN_DK = 128
GDN_DV = 128
GDN_CONV = 4
GDN_QKW = GDN_K_HEADS * GDN_DK
GDN_VW = GDN_V_HEADS * GDN_DV
GDN_CONV_W = 2 * GDN_QKW + GDN_VW
GDN_IN = GDN_CONV_W + GDN_VW + 2 * GDN_V_HEADS
GDN_SPLITS = [GDN_CONV_W, GDN_CONV_W + GDN_VW, GDN_CONV_W + GDN_VW + GDN_V_HEADS]

N_GLA = len(range(0, DEPTH, N_MIXERS))
N_MLA = len(range(1, DEPTH, N_MIXERS))
N_GDN = len(range(2, DEPTH, N_MIXERS))

kernel_name = "chunk_causal_hybrid_gla_mla_gdn_trunk"


def rms_norm(x, g):
    xf = x.astype(jnp.float32)
    y = xf * lax.rsqrt(jnp.mean(xf * xf, axis=-1, keepdims=True) + EPS)
    return (y * g.astype(jnp.float32)).astype(x.dtype)


def l2_normalize(x):
    return x * lax.rsqrt(jnp.sum(x * x, axis=-1, keepdims=True) + EPS)


def to_chunks(t, heads):
    b, s = t.shape[0], t.shape[1]
    return t.reshape(b, s // CHUNK, CHUNK, heads, -1).transpose(1, 0, 3, 2, 4)


def from_chunks(t):
    nc, b, h, l, d = t.shape
    return t.transpose(1, 0, 3, 2, 4).reshape(b, nc * l, h, d)


def gla_mixer(h, w_in, w_gate_up, b_gate, head_g, w_out):
    b, s, _ = h.shape
    f32 = jnp.float32
    q, k, v, r, g_lr = jnp.split(h @ w_in, GLA_SPLITS, axis=-1)
    log_a = jax.nn.log_sigmoid((g_lr @ w_gate_up + b_gate).astype(f32)) / GLA_GATE_TEMP
    qc = to_chunks(q.astype(f32) * GLA_DK ** -0.5, GLA_HEADS)
    kc = to_chunks(k.astype(f32), GLA_HEADS)
    vc = to_chunks(v.astype(f32), GLA_HEADS)
    cum = jnp.cumsum(to_chunks(log_a, GLA_HEADS), axis=3)
    k_end = kc * jnp.exp(cum[..., -1:, :] - cum)
    dec = jnp.exp(cum[..., -1, :])

    def step(state, inp):
        d, ke, vv, qq = inp
        state = d[..., :, None] * state + jnp.einsum('bhlk,bhlv->bhkv', ke, vv)
        return state, jnp.einsum('bhlk,bhkv->bhlv', qq, state)

    s0 = jnp.zeros((b, GLA_HEADS, GLA_DK, GLA_DV), f32)
    _, o = lax.scan(step, s0, (dec, k_end, vc, qc))
    o = rms_norm(from_chunks(o), head_g) * jax.nn.silu(r.astype(f32)).reshape(b, s, GLA_HEADS, GLA_DV)
    return o.reshape(b, s, GLA_VW).astype(h.dtype) @ w_out


def rope_tables(positions):
    half = MLA_ROPE // 2
    inv_freq = ROPE_THETA ** (-jnp.arange(half, dtype=jnp.float32) / half)
    ang = positions.astype(jnp.float32)[..., None] * inv_freq
    return jnp.cos(ang), jnp.sin(ang)


def apply_rope(x, cos, sin):
    x1, x2 = jnp.split(x.astype(jnp.float32), 2, axis=-1)
    return jnp.concatenate([x1 * cos - x2 * sin, x1 * sin + x2 * cos], axis=-1).astype(x.dtype)


def mla_mixer(h, positions, w_in, q_norm_g, w_uq, kv_norm_g, w_ukv, w_out):
    b, s, _ = h.shape
    c_q, c_kv, k_rope = jnp.split(h @ w_in, [MLA_Q_RANK, MLA_Q_RANK + MLA_KV_RANK], axis=-1)
    q = (rms_norm(c_q, q_norm_g) @ w_uq).reshape(b, s, MLA_HEADS, MLA_NOPE + MLA_ROPE)
    kv = (rms_norm(c_kv, kv_norm_g) @ w_ukv).reshape(b, s, MLA_HEADS, MLA_NOPE + MLA_V)
    q_nope, q_rope = jnp.split(q, [MLA_NOPE], axis=-1)
    k_nope, v = jnp.split(kv, [MLA_NOPE], axis=-1)
    cos, sin = rope_tables(positions)
    q_rope = apply_rope(q_rope, cos[:, :, None, :], sin[:, :, None, :])
    k_rope = apply_rope(k_rope, cos, sin)
    scale = (MLA_NOPE + MLA_ROPE) ** -0.5
    n_blk = s // Q_BLOCK
    key_chunk = jnp.arange(s) // CHUNK

    def blocks(t):
        return t.reshape(b, n_blk, Q_BLOCK, MLA_HEADS, -1).transpose(1, 0, 2, 3, 4)

    def attend(args):
        qn, qr, blk = args
        sc = (jnp.einsum('bqhd,bkhd->bhqk', qn, k_nope)
              + jnp.einsum('bqhr,bkr->bhqk', qr, k_rope)).astype(jnp.float32) * scale
        q_chunk = (blk * Q_BLOCK + jnp.arange(Q_BLOCK)) // CHUNK
        mask = key_chunk[None, :] <= q_chunk[:, None]
        p = jax.nn.softmax(jnp.where(mask, sc, NEG_INF), axis=-1).astype(v.dtype)
        return jnp.einsum('bhqk,bkhd->bqhd', p, v)

    o = lax.map(attend, (blocks(q_nope), blocks(q_rope), jnp.arange(n_blk)))
    o = o.transpose(1, 0, 2, 3, 4).reshape(b, s, MLA_HEADS * MLA_V)
    return o @ w_out


def causal_conv(x, w):
    s = x.shape[1]
    xp = jnp.pad(x, ((0, 0), (GDN_CONV - 1, 0), (0, 0)))
    out = xp[:, 0:s] * w[0]
    for j in range(1, GDN_CONV):
        out = out + xp[:, j:j + s] * w[j]
    return out


def gdn_mixer(h, w_in, conv_w, a_log, dt_bias, norm_g, w_out):
    b, s, _ = h.shape
    f32 = jnp.float32
    qkv, z, b_raw, a_raw = jnp.split(h @ w_in, GDN_SPLITS, axis=-1)
    qkv = jax.nn.silu(causal_conv(qkv, conv_w))
    q, k, v = jnp.split(qkv.astype(f32), [GDN_QKW, 2 * GDN_QKW], axis=-1)
    rep = GDN_V_HEADS // GDN_K_HEADS
    q = jnp.repeat(l2_normalize(q.reshape(b, s, GDN_K_HEADS, GDN_DK)), rep, axis=2) * GDN_DK ** -0.5
    k = jnp.repeat(l2_normalize(k.reshape(b, s, GDN_K_HEADS, GDN_DK)), rep, axis=2)
    beta = jax.nn.sigmoid(b_raw.astype(f32))
    g = -jnp.exp(a_log.astype(f32)) * jax.nn.softplus(a_raw.astype(f32) + dt_bias.astype(f32))
    qc = to_chunks(q.reshape(b, s, -1), GDN_V_HEADS)
    kc = to_chunks(k.reshape(b, s, -1), GDN_V_HEADS)
    vc = to_chunks(v, GDN_V_HEADS)
    bc = to_chunks(beta, GDN_V_HEADS)[..., 0]
    cum = jnp.cumsum(to_chunks(g, GDN_V_HEADS)[..., 0], axis=-1)
    strict = jnp.tril(jnp.ones((CHUNK, CHUNK), dtype=bool), -1)
    diff = jnp.where(strict, cum[..., :, None] - cum[..., None, :], 0.0)
    kk = jnp.einsum('nbhik,nbhjk->nbhij', kc, kc)
    a_mat = jnp.where(strict, bc[..., :, None] * jnp.exp(diff) * kk, 0.0)
    lhs = a_mat + jnp.eye(CHUNK, dtype=f32)
    rhs = jnp.concatenate([bc[..., None] * vc, (bc * jnp.exp(cum))[..., None] * kc], axis=-1)
    sol = lax.linalg.triangular_solve(lhs, rhs, left_side=True, lower=True, unit_diagonal=True)
    w_v, w_k = jnp.split(sol, [GDN_DV], axis=-1)
    k_end = kc * jnp.exp(cum[..., -1:] - cum)[..., None]
    dec = jnp.exp(cum[..., -1])

    def step(state, inp):
        wv, wk, ke, d, qq = inp
        u = wv - jnp.einsum('bhlk,bhkv->bhlv', wk, state)
        state = d[..., None, None] * state + jnp.einsum('bhlk,bhlv->bhkv', ke, u)
        return state, jnp.einsum('bhlk,bhkv->bhlv', qq, state)

    s0 = jnp.zeros((b, GDN_V_HEADS, GDN_DK, GDN_DV), f32)
    _, o = lax.scan(step, s0, (w_v, w_k, k_end, dec, qc))
    o = rms_norm(from_chunks(o), norm_g) * jax.nn.silu(z.astype(f32)).reshape(b, s, GDN_V_HEADS, GDN_DV)
    return o.reshape(b, s, GDN_VW).astype(h.dtype) @ w_out


def squared_relu_mlp(h, w_up, w_down):
    return jnp.square(jax.nn.relu(h @ w_up)) @ w_down


def setup_inputs(seed: int = 0) -> dict:
    key = jax.random.key(seed)
    ks = iter(jax.random.split(key, 40))
    f32 = jnp.float32

    def nrm(shape, scale):
        return jax.random.normal(next(ks), shape, f32) * scale

    def gain(shape):
        return 1.0 + nrm(shape, 0.1)

    x = nrm((BATCH, SEQ, D_MODEL), 1.0)
    c = nrm((BATCH, D_MODEL), 1.0)
    positions = jnp.broadcast_to(jnp.arange(SEQ, dtype=jnp.int32), (BATCH, SEQ))
    ada_w = nrm((DEPTH, D_MODEL, N_MOD * D_MODEL), 0.5 * D_MODEL ** -0.5)
    ada_b = nrm((DEPTH, N_MOD * D_MODEL), 0.01)
    norm_pre_g = gain((DEPTH, 2, D_MODEL))
    norm_post_g = gain((DEPTH, 2, D_MODEL))

    gla_w_in = nrm((N_GLA, D_MODEL, GLA_IN), D_MODEL ** -0.5)
    gla_w_gate_up = nrm((N_GLA, GLA_GATE_RANK, GLA_KW), GLA_GATE_RANK ** -0.5)
    gla_b_gate = nrm((N_GLA, GLA_KW), 0.1)
    gla_head_g = gain((N_GLA, GLA_DV))
    gla_w_out = nrm((N_GLA, GLA_VW, D_MODEL), GLA_VW ** -0.5)

    mla_w_in = nrm((N_MLA, D_MODEL, MLA_IN), D_MODEL ** -0.5)
    mla_q_norm_g = gain((N_MLA, MLA_Q_RANK))
    mla_w_uq = nrm((N_MLA, MLA_Q_RANK, MLA_HEADS * (MLA_NOPE + MLA_ROPE)), MLA_Q_RANK ** -0.5)
    mla_kv_norm_g = gain((N_MLA, MLA_KV_RANK))
    mla_w_ukv = nrm((N_MLA, MLA_KV_RANK, MLA_HEADS * (MLA_NOPE + MLA_V)), MLA_KV_RANK ** -0.5)
    mla_w_out = nrm((N_MLA, MLA_HEADS * MLA_V, D_MODEL), (MLA_HEADS * MLA_V) ** -0.5)

    gdn_w_in = nrm((N_GDN, D_MODEL, GDN_IN), D_MODEL ** -0.5)
    gdn_conv_w = nrm((N_GDN, GDN_CONV, GDN_CONV_W), GDN_CONV ** -0.5)
    gdn_a_log = jnp.log(jax.random.uniform(next(ks), (N_GDN, GDN_V_HEADS), f32, 1.0, 16.0))
    dt = jnp.exp(jax.random.uniform(next(ks), (N_GDN, GDN_V_HEADS), f32, math.log(1e-3), math.log(1e-1)))
    gdn_dt_bias = dt + jnp.log(-jnp.expm1(-dt))
    gdn_norm_g = gain((N_GDN, GDN_DV))
    gdn_w_out = nrm((N_GDN, GDN_VW, D_MODEL), GDN_VW ** -0.5)

    mlp_w_up = nrm((DEPTH, D_MODEL, D_FF), D_MODEL ** -0.5)
    mlp_w_down = nrm((DEPTH, D_FF, D_MODEL), D_FF ** -0.5)
    return {
        "x": x, "c": c, "positions": positions,
        "ada_w": ada_w, "ada_b": ada_b, "norm_pre_g": norm_pre_g, "norm_post_g": norm_post_g,
        "gla_w_in": gla_w_in, "gla_w_gate_up": gla_w_gate_up, "gla_b_gate": gla_b_gate,
        "gla_head_g": gla_head_g, "gla_w_out": gla_w_out,
        "mla_w_in": mla_w_in, "mla_q_norm_g": mla_q_norm_g, "mla_w_uq": mla_w_uq,
        "mla_kv_norm_g": mla_kv_norm_g, "mla_w_ukv": mla_w_ukv, "mla_w_out": mla_w_out,
        "gdn_w_in": gdn_w_in, "gdn_conv_w": gdn_conv_w, "gdn_a_log": gdn_a_log,
        "gdn_dt_bias": gdn_dt_bias, "gdn_norm_g": gdn_norm_g, "gdn_w_out": gdn_w_out,
        "mlp_w_up": mlp_w_up, "mlp_w_down": mlp_w_down,
    }


def reference(x, c, positions, ada_w, ada_b, norm_pre_g, norm_post_g,
              gla_w_in, gla_w_gate_up, gla_b_gate, gla_head_g, gla_w_out,
              mla_w_in, mla_q_norm_g, mla_w_uq, mla_kv_norm_g, mla_w_ukv, mla_w_out,
              gdn_w_in, gdn_conv_w, gdn_a_log, gdn_dt_bias, gdn_norm_g, gdn_w_out,
              mlp_w_up, mlp_w_down):
    cond = jax.nn.silu(c)
    for layer in range(DEPTH):
        mod = cond @ ada_w[layer] + ada_b[layer]
        sh_m, sc_m, gt_m, sh_f, sc_f, gt_f = [m[:, None, :] for m in jnp.split(mod, N_MOD, axis=-1)]

        h = rms_norm(x, norm_pre_g[layer, 0]) * (1.0 + sc_m) + sh_m
        kind, j = layer % N_MIXERS, layer // N_MIXERS
        if kind == 0:
            y = gla_mixer(h, gla_w_in[j], gla_w_gate_up[j], gla_b_gate[j], gla_head_g[j], gla_w_out[j])
        elif kind == 1:
            y = mla_mixer(h, positions, mla_w_in[j], mla_q_norm_g[j], mla_w_uq[j],
                          mla_kv_norm_g[j], mla_w_ukv[j], mla_w_out[j])
        else:
            y = gdn_mixer(h, gdn_w_in[j], gdn_conv_w[j], gdn_a_log[j], gdn_dt_bias[j],
                          gdn_norm_g[j], gdn_w_out[j])
        x = x + gt_m * rms_norm(y, norm_post_g[layer, 0])

        h = rms_norm(x, norm_pre_g[layer, 1]) * (1.0 + sc_f) + sh_f
        y = squared_relu_mlp(h, mlp_w_up[layer], mlp_w_down[layer])
        x = x + gt_f * rms_norm(y, norm_post_g[layer, 1])
    return x
```

```python
import functools
import math

import numpy as np
import jax
import jax.numpy as jnp
from jax import lax
from jax.experimental import pallas as pl
from jax.experimental.pallas import tpu as pltpu

F32 = jnp.float32
BF16 = jnp.bfloat16

D_MODEL = 1024
N_MOD = 6
N_MIXERS = 3
CHUNK = 64
EPS = 1e-6
NEG_INF = -1e30
D_FF = 4 * D_MODEL

GLA_HEADS = 4
GLA_DK = 128
GLA_DV = 256
GLA_GATE_RANK = 16
GLA_GATE_TEMP = 16.0
GLA_KW = GLA_HEADS * GLA_DK
GLA_VW = GLA_HEADS * GLA_DV
GLA_IN_PAD = 2 * GLA_KW + 2 * GLA_VW + 128

MLA_HEADS = 16
MLA_NOPE = 64
MLA_ROPE = 32
MLA_V = 64
MLA_Q_RANK = 384
MLA_KV_RANK = 256
MLA_QK = 128
ROPE_THETA = 10000.0

GDN_K_HEADS = 8
GDN_V_HEADS = 16
GDN_DK = 128
GDN_DV = 128
GDN_CONV = 4
GDN_QKW = GDN_K_HEADS * GDN_DK
GDN_VW = GDN_V_HEADS * GDN_DV
GDN_CONV_W = 2 * GDN_QKW + GDN_VW
GDN_Z_PAD = GDN_VW + 256

VMEM_LIMIT = 56 * 1024 * 1024

_TN = (((0,), (0,)), ((), ()))
_NT = (((1,), (1,)), ((), ()))


def _params(n_axes):
    return pltpu.CompilerParams(dimension_semantics=("arbitrary",) * n_axes, vmem_limit_bytes=VMEM_LIMIT)


def _dot(a, b):
    return jnp.dot(a, b, preferred_element_type=F32)


def _rms(x, g):
    return x * lax.rsqrt(jnp.mean(x * x, axis=-1, keepdims=True) + EPS) * g


def _silu(x):
    return x * jax.nn.sigmoid(x)


def _split2(x):
    hi = x.astype(BF16)
    lo = (x - hi.astype(F32)).astype(BF16)
    return hi, lo


def _split3(x):
    hi = x.astype(BF16)
    r = x - hi.astype(F32)
    mid = r.astype(BF16)
    lo = (r - mid.astype(F32)).astype(BF16)
    return hi, mid, lo


def _ada_kernel(c_ref, w_ref, b_ref, o_ref):
    cond = _silu(c_ref[...]).astype(BF16)
    o_ref[0] = _dot(cond, w_ref[0].astype(BF16)) + b_ref[0]


def _ada_mod(c, ada_w, ada_b):
    b, d = c.shape
    depth, _, nm = ada_w.shape
    tn = 1536
    cp = jnp.zeros((8, d), F32).at[:b].set(c)
    return pl.pallas_call(
        _ada_kernel,
        grid=(depth, nm // tn),
        in_specs=[pl.BlockSpec((8, d), lambda l, j: (0, 0)),
                  pl.BlockSpec((1, d, tn), lambda l, j: (l, 0, j)),
                  pl.BlockSpec((1, 1, tn), lambda l, j: (l, 0, j))],
        out_specs=pl.BlockSpec((1, 8, tn), lambda l, j: (l, 0, j)),
        out_shape=jax.ShapeDtypeStruct((depth, 8, nm), F32),
        compiler_params=_params(2),
        name="ada_mod",
    )(cp, ada_w, ada_b.reshape(depth, 1, nm))


def _prenorm(x_ref, g_ref, sc_ref, sh_ref, h_ref):
    h = _rms(x_ref[...], g_ref[...]) * (1.0 + sc_ref[0]) + sh_ref[0]
    h_ref[...] = h.astype(BF16)


def _prenorm_matmul_kernel(x_ref, g_ref, sc_ref, sh_ref, w_ref, o_ref, h_ref):
    @pl.when(pl.program_id(1) == 0)
    def _():
        _prenorm(x_ref, g_ref, sc_ref, sh_ref, h_ref)

    o_ref[...] = _dot(h_ref[...], w_ref[...]).astype(o_ref.dtype)


def _prenorm_specs(tm, d, tpb):
    return [pl.BlockSpec((tm, d), lambda i, j: (i, 0)),
            pl.BlockSpec((1, d), lambda i, j: (0, 0)),
            pl.BlockSpec((1, 1, d), lambda i, j: (i // tpb, 0, 0)),
            pl.BlockSpec((1, 1, d), lambda i, j: (i // tpb, 0, 0))]


def _prenorm_matmul(x, g, sc, sh, w, *, seq, tm, tn, name):
    n, d = x.shape
    nout = w.shape[1]
    return pl.pallas_call(
        _prenorm_matmul_kernel,
        grid=(n // tm, nout // tn),
        in_specs=_prenorm_specs(tm, d, seq // tm) + [pl.BlockSpec((d, tn), lambda i, j: (0, j))],
        out_specs=pl.BlockSpec((tm, tn), lambda i, j: (i, j)),
        out_shape=jax.ShapeDtypeStruct((n, nout), BF16),
        scratch_shapes=[pltpu.VMEM((tm, d), BF16)],
        compiler_params=_params(2),
        name=name,
    )(x, g, sc, sh, w)


def _out_res_kernel(a_ref, w_ref, x_ref, g_ref, gt_ref, o_ref):
    y = _dot(a_ref[...], w_ref[...])
    o_ref[...] = x_ref[...] + gt_ref[0] * _rms(y, g_ref[...])


def _out_res(a, w, x, g, gt, *, seq, tm, name):
    n, k = a.shape
    d = w.shape[1]
    tpb = seq // tm
    return pl.pallas_call(
        _out_res_kernel,
        grid=(n // tm,),
        in_specs=[pl.BlockSpec((tm, k), lambda i: (i, 0)),
                  pl.BlockSpec((k, d), lambda i: (0, 0)),
                  pl.BlockSpec((tm, d), lambda i: (i, 0)),
                  pl.BlockSpec((1, d), lambda i: (0, 0)),
                  pl.BlockSpec((1, 1, d), lambda i: (i // tpb, 0, 0))],
        out_specs=pl.BlockSpec((tm, d), lambda i: (i, 0)),
        out_shape=jax.ShapeDtypeStruct((n, d), F32),
        compiler_params=_params(1),
        name=name,
    )(a, w, x, g, gt)


def _mlp_kernel(x_ref, g1_ref, sc_ref, sh_ref, wu_ref, wd_ref, g2_ref, gt_ref, o_ref, h_ref, acc_ref, *, nf):
    f = pl.program_id(1)

    @pl.when(f == 0)
    def _():
        _prenorm(x_ref, g1_ref, sc_ref, sh_ref, h_ref)
        acc_ref[...] = jnp.zeros_like(acc_ref)

    u = jnp.square(jnp.maximum(_dot(h_ref[...], wu_ref[...]), 0.0)).astype(BF16)
    acc_ref[...] += _dot(u, wd_ref[...])

    @pl.when(f == nf - 1)
    def _():
        o_ref[...] = x_ref[...] + gt_ref[0] * _rms(acc_ref[...], g2_ref[...])


def _mlp(x, g1, sc, sh, wu, wd, g2, gt, *, seq, tm, tf):
    n, d = x.shape
    ff = wu.shape[1]
    nf = ff // tf
    tpb = seq // tm
    return pl.pallas_call(
        functools.partial(_mlp_kernel, nf=nf),
        grid=(n // tm, nf),
        in_specs=_prenorm_specs(tm, d, tpb) + [
            pl.BlockSpec((d, tf), lambda i, f: (0, f)),
            pl.BlockSpec((tf, d), lambda i, f: (f, 0)),
            pl.BlockSpec((1, d), lambda i, f: (0, 0)),
            pl.BlockSpec((1, 1, d), lambda i, f: (i // tpb, 0, 0))],
        out_specs=pl.BlockSpec((tm, d), lambda i, f: (i, 0)),
        out_shape=jax.ShapeDtypeStruct((n, d), F32),
        scratch_shapes=[pltpu.VMEM((tm, d), BF16), pltpu.VMEM((tm, d), F32)],
        compiler_params=_params(2),
        name="mlp",
    )(x, g1, sc, sh, wu, wd, g2, gt)


def _gla_core_kernel(q_ref, k_ref, v_ref, r_ref, glr_ref, wgu_ref, bg_ref, hg_ref, o_ref, st_ref, la_ref, *, cg):
    @pl.when(pl.program_id(2) == 0)
    def _():
        st_ref[...] = jnp.zeros_like(st_ref)

    z = _dot(glr_ref[...], wgu_ref[...]) + bg_ref[...]
    la_ref[...] = jax.nn.log_sigmoid(z) * (1.0 / GLA_GATE_TEMP)
    ii = lax.broadcasted_iota(jnp.int32, (CHUNK, CHUNK), 0)
    jj = lax.broadcasted_iota(jnp.int32, (CHUNK, CHUNK), 1)
    tri = (ii >= jj).astype(BF16)
    hg = hg_ref[...]
    for c in range(cg):
        rows = slice(c * CHUNK, (c + 1) * CHUNK)
        hi, lo = _split2(la_ref[rows, :])
        cum = _dot(tri, hi) + _dot(tri, lo)
        tot = cum[CHUNK - 1:CHUNK, :]
        ke = (k_ref[rows, :].astype(F32) * jnp.exp(tot - cum)).astype(BF16)
        kv_t = lax.dot_general(v_ref[rows, :], ke, _TN, preferred_element_type=F32)
        st = st_ref[...] * jnp.exp(tot) + kv_t
        st_ref[...] = st
        o = lax.dot_general(q_ref[rows, :], st.astype(BF16), _NT, preferred_element_type=F32) * (GLA_DK ** -0.5)
        rr = r_ref[rows, :].astype(F32)
        o_ref[rows, :] = (_rms(o, hg) * _silu(rr)).astype(BF16)


def _gla_core(y, wgu, bg, hg, *, batch, seq, cg):
    n = y.shape[0]
    rws = cg * CHUNK
    ng = seq // rws
    kb, vb = GLA_KW // GLA_DK, (2 * GLA_KW) // GLA_DV
    return pl.pallas_call(
        functools.partial(_gla_core_kernel, cg=cg),
        grid=(batch, GLA_HEADS, ng),
        in_specs=[pl.BlockSpec((rws, GLA_DK), lambda b, h, g: (b * ng + g, h)),
                  pl.BlockSpec((rws, GLA_DK), lambda b, h, g: (b * ng + g, kb + h)),
                  pl.BlockSpec((rws, GLA_DV), lambda b, h, g: (b * ng + g, vb + h)),
                  pl.BlockSpec((rws, GLA_DV), lambda b, h, g: (b * ng + g, vb + GLA_HEADS + h)),
                  pl.BlockSpec((rws, 128), lambda b, h, g: (b * ng + g, (2 * GLA_KW + 2 * GLA_VW) // 128)),
                  pl.BlockSpec((128, GLA_DK), lambda b, h, g: (0, h)),
                  pl.BlockSpec((1, GLA_DK), lambda b, h, g: (0, h)),
                  pl.BlockSpec((1, GLA_DV), lambda b, h, g: (0, 0))],
        out_specs=pl.BlockSpec((rws, GLA_DV), lambda b, h, g: (b * ng + g, h)),
        out_shape=jax.ShapeDtypeStruct((n, GLA_VW), BF16),
        scratch_shapes=[pltpu.VMEM((GLA_DV, GLA_DK), F32), pltpu.VMEM((rws, GLA_DK), F32)],
        compiler_params=_params(3),
        name="gla_core",
    )(y, y, y, y, y, wgu, bg, hg)


def _gla_layer(x, pre_g, sc, sh, gt, post_g, w_in, w_gate_up, b_gate, head_g, w_out, *, batch, seq):
    d = x.shape[1]
    w = jnp.pad(w_in, ((0, 0), (0, GLA_IN_PAD - w_in.shape[1]))).astype(BF16)
    wgu = jnp.pad(w_gate_up, ((0, 128 - GLA_GATE_RANK), (0, 0))).astype(BF16)
    y = _prenorm_matmul(x, pre_g.reshape(1, d), sc, sh, w, seq=seq, tm=min(512, seq), tn=GLA_IN_PAD, name="gla_in")
    o = _gla_core(y, wgu, b_gate.reshape(1, GLA_KW), head_g.reshape(1, GLA_DV), batch=batch, seq=seq, cg=8)
    return _out_res(o, w_out.astype(BF16), x, post_g.reshape(1, d), gt, seq=seq, tm=min(512, seq), name="gla_out")


def _rope_kernel(pos_ref, invf_ref, cos_ref, sin_ref):
    ang = invf_ref[...] * pos_ref[...].astype(F32)
    cos_ref[...] = jnp.cos(ang)
    sin_ref[...] = jnp.sin(ang)


def _rope_table(positions):
    n = positions.size
    half = MLA_ROPE // 2
    tn = min(2048, n)
    invf = jnp.asarray((ROPE_THETA ** (-np.arange(half, dtype=np.float32) / half)).reshape(half, 1), F32)
    cos_t, sin_t = pl.pallas_call(
        _rope_kernel,
        grid=(n // tn,),
        in_specs=[pl.BlockSpec((1, tn), lambda i: (0, i)), pl.BlockSpec((half, 1), lambda i: (0, 0))],
        out_specs=[pl.BlockSpec((half, tn), lambda i: (0, i))] * 2,
        out_shape=[jax.ShapeDtypeStruct((half, n), F32)] * 2,
        compiler_params=_params(1),
        name="rope_table",
    )(positions.reshape(1, n), invf)
    cos, sin = cos_t.T, sin_t.T
    return jnp.concatenate([jnp.ones((n, MLA_NOPE), F32), cos, sin, sin, cos], axis=-1)


def _mla_q_kernel(c_ref, g_ref, w_ref, t_ref, o_ref):
    rn = _rms(c_ref[:, :MLA_Q_RANK].astype(F32), g_ref[...]).astype(BF16)
    a = _dot(rn, w_ref[...])
    t = t_ref[...] * ((MLA_NOPE + MLA_ROPE) ** -0.5)
    for h in range(MLA_HEADS):
        cols = slice(h * MLA_QK, (h + 1) * MLA_QK)
        o_ref[:, cols] = (a[:, cols] * t).astype(BF16)


def _mla_kv_kernel(c_ref, kr_ref, g_ref, wk_ref, wv_ref, t_ref, p_ref, ko_ref, vo_ref):
    rn = _rms(c_ref[...].astype(F32), g_ref[...]).astype(BF16)
    kr = _dot((kr_ref[...].astype(F32) * t_ref[...]).astype(BF16), p_ref[...])
    ak = _dot(rn, wk_ref[...])
    for h in range(MLA_HEADS):
        cols = slice(h * MLA_QK, (h + 1) * MLA_QK)
        ko_ref[:, cols] = (ak[:, cols] + kr).astype(BF16)
    vo_ref[...] = _dot(rn, wv_ref[...]).astype(BF16)


def _mla_attn_kernel(q_ref, k_ref, v_ref, o_ref, m_ref, l_ref, acc_ref, *, tq):
    i = pl.program_id(2)
    low = lax.broadcasted_iota(jnp.int32, (1, 128), 1) < MLA_V
    m_ref[...] = jnp.full_like(m_ref, NEG_INF)
    l_ref[...] = jnp.zeros_like(l_ref)
    acc_ref[...] = jnp.zeros_like(acc_ref)
    rr = lax.broadcasted_iota(jnp.int32, (tq, tq), 0)
    cc = lax.broadcasted_iota(jnp.int32, (tq, tq), 1)
    diag_mask = (rr >> 6) >= (cc >> 6)

    def block(j, masked):
        rows = pl.ds(pl.multiple_of(j * tq, tq), tq)
        alphas, pvs = [], []
        for hh in range(2):
            cols = slice(hh * MLA_QK, (hh + 1) * MLA_QK)
            s = lax.dot_general(q_ref[:, cols], k_ref[rows, cols], _NT, preferred_element_type=F32)
            if masked:
                s = jnp.where(diag_mask, s, NEG_INF)
            m_prev = m_ref[hh]
            m_new = jnp.maximum(m_prev, jnp.max(s, axis=-1, keepdims=True))
            alpha = jnp.exp(m_prev - m_new)
            p = jnp.exp(s - pltpu.repeat(m_new, tq // 128, axis=1))
            l_ref[hh] = alpha * l_ref[hh] + jnp.sum(p, axis=-1, keepdims=True)
            m_ref[hh] = m_new
            alphas.append(alpha)
            pvs.append(_dot(p.astype(BF16), v_ref[rows, cols]))
        acc_ref[...] = acc_ref[...] * jnp.where(low, alphas[0], alphas[1]) + pvs[0] + pvs[1]

    def body(j, carry):
        block(j, False)
        return carry

    lax.fori_loop(0, i, body, 0)
    block(i, True)
    o_ref[...] = (acc_ref[...] * jnp.where(low, 1.0 / l_ref[0], 1.0 / l_ref[1])).astype(BF16)


def _mla_rope_mix():
    p = np.zeros((128, 128), np.float32)
    for i in range(16):
        a, b, c, d = 64 + i, 80 + i, 96 + i, 112 + i
        p[a, a], p[b, a] = 1.0, -1.0
        p[a, b], p[b, b] = -1.0, 1.0
        p[c, c] = p[d, c] = 1.0
        p[c, d] = p[d, d] = 1.0
    return jnp.asarray(p, BF16)


def _mla_layer(x, positions, pre_g, sc, sh, gt, post_g, w_in, q_norm_g, w_uq, kv_norm_g, w_ukv, w_out, *, batch, seq):
    n, d = x.shape
    hq = MLA_HEADS * MLA_QK
    tm = min(512, seq)
    wq, wkv, wr = w_in[:, :MLA_Q_RANK], w_in[:, MLA_Q_RANK:MLA_Q_RANK + MLA_KV_RANK], w_in[:, MLA_Q_RANK + MLA_KV_RANK:]
    w1, w2 = wr[:, :16], wr[:, 16:]
    z = lambda k: jnp.zeros((d, k), F32)
    w_in_p = jnp.concatenate([wq, z(128), wkv, z(64), w1, w2, w1, w2, z(128)], axis=1).astype(BF16)
    y = _prenorm_matmul(x, pre_g.reshape(1, d), sc, sh, w_in_p, seq=seq, tm=tm, tn=1024, name="mla_in")

    table = _rope_table(positions)

    uq = w_uq.reshape(MLA_Q_RANK, MLA_HEADS, MLA_NOPE + MLA_ROPE)
    x1, x2 = uq[:, :, 64:80], uq[:, :, 80:96]
    wq_p = jnp.concatenate([uq[:, :, :64], x1, x2, x1, x2], axis=-1).reshape(MLA_Q_RANK, hq).astype(BF16)
    q = pl.pallas_call(
        _mla_q_kernel,
        grid=(n // tm,),
        in_specs=[pl.BlockSpec((tm, 512), lambda i: (i, 0)),
                  pl.BlockSpec((1, MLA_Q_RANK), lambda i: (0, 0)),
                  pl.BlockSpec((MLA_Q_RANK, hq), lambda i: (0, 0)),
                  pl.BlockSpec((tm, 128), lambda i: (i, 0))],
        out_specs=pl.BlockSpec((tm, hq), lambda i: (i, 0)),
        out_shape=jax.ShapeDtypeStruct((n, hq), BF16),
        compiler_params=_params(1),
        name="mla_q",
    )(y, q_norm_g.reshape(1, MLA_Q_RANK), wq_p, table)

    ukv = w_ukv.reshape(MLA_KV_RANK, MLA_HEADS, MLA_NOPE + MLA_V)
    wk_p = jnp.concatenate([ukv[:, :, :64], jnp.zeros((MLA_KV_RANK, MLA_HEADS, 64), F32)], axis=-1)
    wk_p = wk_p.reshape(MLA_KV_RANK, hq).astype(BF16)
    uv = ukv[:, :, 64:].reshape(MLA_KV_RANK, MLA_HEADS // 2, 2, MLA_V)
    zv = jnp.zeros_like(uv[:, :, 0])
    wv_p = jnp.stack([jnp.concatenate([uv[:, :, 0], zv], -1), jnp.concatenate([zv, uv[:, :, 1]], -1)], axis=2)
    wv_p = wv_p.reshape(MLA_KV_RANK, hq).astype(BF16)
    k, v = pl.pallas_call(
        _mla_kv_kernel,
        grid=(n // tm,),
        in_specs=[pl.BlockSpec((tm, MLA_KV_RANK), lambda i: (i, 2)),
                  pl.BlockSpec((tm, 128), lambda i: (i, 6)),
                  pl.BlockSpec((1, MLA_KV_RANK), lambda i: (0, 0)),
                  pl.BlockSpec((MLA_KV_RANK, hq), lambda i: (0, 0)),
                  pl.BlockSpec((MLA_KV_RANK, hq), lambda i: (0, 0)),
                  pl.BlockSpec((tm, 128), lambda i: (i, 0)),
                  pl.BlockSpec((128, 128), lambda i: (0, 0))],
        out_specs=[pl.BlockSpec((tm, hq), lambda i: (i, 0))] * 2,
        out_shape=[jax.ShapeDtypeStruct((n, hq), BF16)] * 2,
        compiler_params=_params(1),
        name="mla_kv",
    )(y, y, kv_norm_g.reshape(1, MLA_KV_RANK), wk_p, wv_p, table, _mla_rope_mix())

    tq = min(512, seq)
    nq = seq // tq
    pw = 2 * MLA_QK
    o = pl.pallas_call(
        functools.partial(_mla_attn_kernel, tq=tq),
        grid=(batch, MLA_HEADS // 2, nq),
        in_specs=[pl.BlockSpec((tq, pw), lambda b, p, i: (b * nq + i, p)),
                  pl.BlockSpec((seq, pw), lambda b, p, i: (b, p)),
                  pl.BlockSpec((seq, pw), lambda b, p, i: (b, p))],
        out_specs=pl.BlockSpec((tq, 2 * MLA_V), lambda b, p, i: (b * nq + i, p)),
        out_shape=jax.ShapeDtypeStruct((n, MLA_HEADS * MLA_V), BF16),
        scratch_shapes=[pltpu.VMEM((2, tq, 128), F32), pltpu.VMEM((2, tq, 128), F32), pltpu.VMEM((tq, 128), F32)],
        compiler_params=_params(3),
        name="mla_attn",
    )(q, k, v)
    return _out_res(o, w_out.astype(BF16), x, post_g.reshape(1, d), gt, seq=seq, tm=tm, name="mla_out")


def _gdn_conv_kernel(x_ref, g_ref, sc_ref, sh_ref, w_ref, cw_ref, o_ref, h_ref, xb_ref, carry_ref, *, tm, tpb, n_qk):
    i, j = pl.program_id(0), pl.program_id(1)

    @pl.when(j == 0)
    def _():
        _prenorm(x_ref, g_ref, sc_ref, sh_ref, h_ref)

    acc = _dot(h_ref[...], w_ref[...])
    first = (i % tpb) == 0

    @pl.when(first)
    def _():
        xb_ref[0:8, :] = jnp.zeros((8, acc.shape[1]), F32)

    @pl.when(jnp.logical_not(first))
    def _():
        xb_ref[0:8, :] = carry_ref[j]

    xb_ref[8:8 + tm, :] = acc
    carry_ref[j] = acc[tm - 8:tm, :]
    cw = cw_ref[...]
    y = (cw[3:4] * acc + cw[2:3] * xb_ref[7:7 + tm, :] + cw[1:2] * xb_ref[6:6 + tm, :] + cw[0:1] * xb_ref[5:5 + tm, :])
    y = _silu(y)

    @pl.when(j < n_qk)
    def _():
        for s in range(0, y.shape[1], GDN_DK):
            ys = y[:, s:s + GDN_DK]
            o_ref[:, s:s + GDN_DK] = (ys * lax.rsqrt(jnp.sum(ys * ys, axis=-1, keepdims=True) + EPS)).astype(BF16)

    @pl.when(j >= n_qk)
    def _():
        o_ref[...] = y.astype(BF16)


def _gdn_gate_kernel(alog_ref, dtb_ref, b_ref, a_ref, beta_ref, cum_ref, cp_ref, be_ref, e3_ref, dec_ref):
    h = pl.program_id(1)
    b, a = b_ref[0, 0], a_ref[0, 0]
    beta = jax.nn.sigmoid(b)
    g = -jnp.exp(jnp.full(a.shape, alog_ref[h], F32)) * jax.nn.softplus(a + dtb_ref[h])
    ii = lax.broadcasted_iota(jnp.int32, (CHUNK, CHUNK), 0)
    jj = lax.broadcasted_iota(jnp.int32, (CHUNK, CHUNK), 1)
    upper = (ii <= jj).astype(BF16)
    ones = jnp.ones((CHUNK, 128), BF16)
    parts = _split3(g)
    cum = sum(_dot(p, upper) for p in parts)
    tot = sum(_dot(p, ones) for p in parts)
    beta_ref[0, 0] = beta
    cum_ref[0, 0] = cum
    cp_ref[0, 0] = cum + jax.nn.log_sigmoid(b)
    be_ref[0, 0] = beta * jnp.exp(cum)
    e3_ref[0, 0] = jnp.exp(tot[:, :CHUNK] - cum)
    dec_ref[0, 0] = jnp.exp(tot)


def _unit_lower_inverse(a, eye, m8, offs):
    mm = lambda p, q: _dot(p.astype(BF16), q.astype(BF16))
    nb = jnp.where(m8, -a, 0.0)
    n2 = mm(nb, nb)
    t = eye + nb
    t = t + mm(t, n2)
    t = t + mm(t, mm(n2, n2))
    for off in offs:
        t = t - mm(t, mm(jnp.where(off, a, 0.0), t))
    return t


def _gdn_core_kernel(q_ref, k_ref, v_ref, z_ref, beta_ref, cum_ref, cp_ref, be_ref, e3_ref, dec_ref, ng_ref,
                     o_ref, st_ref, *, cg):
    @pl.when(pl.program_id(2) == 0)
    def _():
        st_ref[...] = jnp.zeros_like(st_ref)

    ii = lax.broadcasted_iota(jnp.int32, (CHUNK, CHUNK), 0)
    jj = lax.broadcasted_iota(jnp.int32, (CHUNK, CHUNK), 1)
    strict = ii > jj
    eye = (ii == jj).astype(F32)
    m8 = (ii >> 3) == (jj >> 3)
    offs = [((ii >> (s + 1)) == (jj >> (s + 1))) & ((ii >> s) != (jj >> s)) for s in (3, 4, 5)]
    ng = ng_ref[...]

    def chunk(c, carry):
        rows = pl.ds(pl.multiple_of(c * CHUNK, CHUNK), CHUNK)
        crow = pl.ds(c, 1)
        kc = k_ref[rows, :]
        qc = q_ref[rows, :]
        kk = lax.dot_general(kc, kc, _NT, preferred_element_type=F32)
        k_t = kc.astype(F32).T
        for e in range(2):
            cols = slice(e * GDN_DV, (e + 1) * GDN_DV)
            rowb = lambda ref: jnp.broadcast_to(ref[0, e, crow, :], (CHUNK, CHUNK))
            diff = rowb(cp_ref).T - rowb(cum_ref)
            a = jnp.where(strict, jnp.exp(diff) * kk, 0.0)
            t = _unit_lower_inverse(a, eye, m8, offs)
            w_v = _dot((t * rowb(beta_ref)).astype(BF16), v_ref[rows, cols])
            w_k = _dot((t * rowb(be_ref)).astype(BF16), kc)
            s_prev = st_ref[e]
            u = w_v - _dot(w_k.astype(BF16), s_prev.astype(BF16))
            ke_t = (k_t * jnp.broadcast_to(e3_ref[0, e, crow, :], (GDN_DK, CHUNK))).astype(BF16)
            s_new = s_prev * jnp.broadcast_to(dec_ref[0, e, crow, :], (GDN_DK, GDN_DV)) + _dot(ke_t, u.astype(BF16))
            st_ref[e] = s_new
            o = _dot(qc, s_new.astype(BF16)) * (GDN_DK ** -0.5)
            zz = z_ref[rows, cols].astype(F32)
            o_ref[rows, cols] = (_rms(o, ng) * _silu(zz)).astype(BF16)
        return carry

    lax.fori_loop(0, cg, chunk, 0)


def _gdn_layer(x, pre_g, sc, sh, gt, post_g, w_in, conv_w, a_log, dt_bias, norm_g, w_out, *, batch, seq):
    n, d = x.shape
    tm = min(512, seq)
    tn = 512
    tpb = seq // tm
    pre_g = pre_g.reshape(1, d)
    n_ct = GDN_CONV_W // tn
    qkv = pl.pallas_call(
        functools.partial(_gdn_conv_kernel, tm=tm, tpb=tpb, n_qk=2 * GDN_QKW // tn),
        grid=(n // tm, n_ct),
        in_specs=_prenorm_specs(tm, d, tpb) + [pl.BlockSpec((d, tn), lambda i, j: (0, j)),
                                               pl.BlockSpec((GDN_CONV, tn), lambda i, j: (0, j))],
        out_specs=pl.BlockSpec((tm, tn), lambda i, j: (i, j)),
        out_shape=jax.ShapeDtypeStruct((n, GDN_CONV_W), BF16),
        scratch_shapes=[pltpu.VMEM((tm, d), BF16), pltpu.VMEM((tm + 8, tn), F32), pltpu.VMEM((n_ct, 8, tn), F32)],
        compiler_params=_params(2),
        name="gdn_conv_in",
    )(x, pre_g, sc, sh, w_in[:, :GDN_CONV_W].astype(BF16), conv_w)

    w_z = jnp.pad(w_in[:, GDN_CONV_W:], ((0, 0), (0, GDN_Z_PAD - (w_in.shape[1] - GDN_CONV_W)))).astype(BF16)
    zb = _prenorm_matmul(x, pre_g, sc, sh, w_z, seq=seq, tm=tm, tn=GDN_Z_PAD // 3, name="gdn_z_in")

    nc = seq // CHUNK
    ba = zb[:, GDN_VW:GDN_VW + 2 * GDN_V_HEADS].astype(F32).reshape(batch, nc, CHUNK, 2 * GDN_V_HEADS)
    ba = ba.transpose(0, 3, 1, 2)
    gate_spec = pl.BlockSpec((1, 1, nc, CHUNK), lambda b, h: (b, h, 0, 0))
    gshape = jax.ShapeDtypeStruct((batch, GDN_V_HEADS, nc, CHUNK), F32)
    smem = pl.BlockSpec(memory_space=pltpu.SMEM)
    beta, cum, cp, be, e3, dec = pl.pallas_call(
        _gdn_gate_kernel,
        grid=(batch, GDN_V_HEADS),
        in_specs=[smem, smem, gate_spec, pl.BlockSpec((1, 1, nc, CHUNK), lambda b, h: (b, GDN_V_HEADS + h, 0, 0))],
        out_specs=[gate_spec] * 5 + [pl.BlockSpec((1, 1, nc, 128), lambda b, h: (b, h, 0, 0))],
        out_shape=[gshape] * 5 + [jax.ShapeDtypeStruct((batch, GDN_V_HEADS, nc, 128), F32)],
        compiler_params=_params(2),
        name="gdn_gate",
    )(a_log, dt_bias, ba, ba)

    cg = 8
    rws = cg * CHUNK
    ng = seq // rws
    kb = GDN_QKW // GDN_DK
    vb = 2 * GDN_QKW // (2 * GDN_DV)
    pair = pl.BlockSpec((1, 2, cg, CHUNK), lambda b, h, g: (b, h, g, 0))
    o = pl.pallas_call(
        functools.partial(_gdn_core_kernel, cg=cg),
        grid=(batch, GDN_K_HEADS, ng),
        in_specs=[pl.BlockSpec((rws, GDN_DK), lambda b, h, g: (b * ng + g, h)),
                  pl.BlockSpec((rws, GDN_DK), lambda b, h, g: (b * ng + g, kb + h)),
                  pl.BlockSpec((rws, 2 * GDN_DV), lambda b, h, g: (b * ng + g, vb + h)),
                  pl.BlockSpec((rws, 2 * GDN_DV), lambda b, h, g: (b * ng + g, h)),
                  pair, pair, pair, pair, pair,
                  pl.BlockSpec((1, 2, cg, 128), lambda b, h, g: (b, h, g, 0)),
                  pl.BlockSpec((1, GDN_DV), lambda b, h, g: (0, 0))],
        out_specs=pl.BlockSpec((rws, 2 * GDN_DV), lambda b, h, g: (b * ng + g, h)),
        out_shape=jax.ShapeDtypeStruct((n, GDN_VW), BF16),
        scratch_shapes=[pltpu.VMEM((2, GDN_DK, GDN_DV), F32)],
        compiler_params=_params(3),
        name="gdn_core",
    )(qkv, qkv, qkv, zb, beta, cum, cp, be, e3, dec, norm_g.reshape(1, GDN_DV))
    return _out_res(o, w_out.astype(BF16), x, post_g.reshape(1, d), gt, seq=seq, tm=tm, name="gdn_out")


def kernel(x, c, positions, ada_w, ada_b, norm_pre_g, norm_post_g, gla_w_in, gla_w_gate_up, gla_b_gate, gla_head_g, gla_w_out, mla_w_in, mla_q_norm_g, mla_w_uq, mla_kv_norm_g, mla_w_ukv, mla_w_out, gdn_w_in, gdn_conv_w, gdn_a_log, gdn_dt_bias, gdn_norm_g, gdn_w_out, mlp_w_up, mlp_w_down):
    batch, seq, d = x.shape
    depth = ada_w.shape[0]
    xf = x.reshape(batch * seq, d)
    mod = _ada_mod(c, ada_w, ada_b)
    for layer in range(depth):
        m = mod[layer, :batch]
        sh_m, sc_m, gt_m, sh_f, sc_f, gt_f = [m[:, k * d:(k + 1) * d].reshape(batch, 1, d) for k in range(N_MOD)]
        kind, j = layer % N_MIXERS, layer // N_MIXERS
        pre, post = norm_pre_g[layer, 0], norm_post_g[layer, 0]
        if kind == 0:
            xf = _gla_layer(xf, pre, sc_m, sh_m, gt_m, post, gla_w_in[j], gla_w_gate_up[j], gla_b_gate[j],
                            gla_head_g[j], gla_w_out[j], batch=batch, seq=seq)
        elif kind == 1:
            xf = _mla_layer(xf, positions, pre, sc_m, sh_m, gt_m, post, mla_w_in[j], mla_q_norm_g[j], mla_w_uq[j],
                            mla_kv_norm_g[j], mla_w_ukv[j], mla_w_out[j], batch=batch, seq=seq)
        else:
            xf = _gdn_layer(xf, pre, sc_m, sh_m, gt_m, post, gdn_w_in[j], gdn_conv_w[j], gdn_a_log[j],
                            gdn_dt_bias[j], gdn_norm_g[j], gdn_w_out[j], batch=batch, seq=seq)
        xf = _mlp(xf, norm_pre_g[layer, 1].reshape(1, d), sc_f, sh_f, mlp_w_up[layer].astype(BF16),
                  mlp_w_down[layer].astype(BF16), norm_post_g[layer, 1].reshape(1, d), gt_f,
                  seq=seq, tm=min(1024, seq), tf=512)
    return xf.reshape(batch, seq, d)
```
